```python
import math
import jax, jax.numpy as jnp
from jax import lax
import numpy as np

D_MODEL = 1024
BATCH = 4
SEQ = 4096
DEPTH = 2

N_MIXERS = 2
MLSTM_HEADS = 4
MLSTM_DQK = D_MODEL // (2 * MLSTM_HEADS)
MLSTM_DV = D_MODEL // MLSTM_HEADS
MLSTM_QK = MLSTM_HEADS * MLSTM_DQK
MLSTM_V = MLSTM_HEADS * MLSTM_DV
MLSTM_IN = 2 * MLSTM_QK + 2 * MLSTM_V + 2 * MLSTM_HEADS
CHUNK = 64
SCONV_WIDTH = D_MODEL
CONV_K = 3
D_FF = 2816
EPS = 1e-6

kernel_name = "xlstm_shortconv_convffn_hybrid"


def rmsnorm(x, g):
    xf = x.astype(jnp.float32)
    y = xf * lax.rsqrt(jnp.mean(xf * xf, axis=-1, keepdims=True) + EPS) * g.astype(jnp.float32)
    return y.astype(x.dtype)


def causal_dwconv(x, w, b):
    c = x.shape[-1]
    rhs = w.astype(x.dtype)[:, None, :]
    y = lax.conv_general_dilated(
        x, rhs, window_strides=(1,), padding=[(CONV_K - 1, 0)],
        dimension_numbers=('NWC', 'WIO', 'NWC'), feature_group_count=c)
    return y + b.astype(x.dtype)


def mlstm_mixer(x, w_in, b_gates, head_norm, w_out):
    bsz, s, _ = x.shape
    h_, dqk, dv, L = MLSTM_HEADS, MLSTM_DQK, MLSTM_DV, CHUNK
    nc = s // L
    proj = x @ w_in
    i0 = MLSTM_QK
    i1 = i0 + MLSTM_QK
    i2 = i1 + MLSTM_V
    i3 = i2 + MLSTM_V
    i4 = i3 + MLSTM_HEADS
    q = proj[..., :i0].astype(jnp.float32) * (dqk ** -0.5)
    k = proj[..., i0:i1].astype(jnp.float32)
    v = proj[..., i1:i2].astype(jnp.float32)
    o_gate = jax.nn.sigmoid(proj[..., i2:i3].astype(jnp.float32))
    bg = b_gates.astype(jnp.float32)
    log_i = proj[..., i3:i4].astype(jnp.float32) + bg[:MLSTM_HEADS]
    log_f = jax.nn.log_sigmoid(proj[..., i4:].astype(jnp.float32) + bg[MLSTM_HEADS:])

    def to_chunks(t, d):
        return t.reshape(bsz, nc, L, h_, d).transpose(1, 0, 3, 2, 4)

    qc = to_chunks(q, dqk)
    kc = to_chunks(k, dqk)
    vc = to_chunks(v, dv)
    lic = log_i.reshape(bsz, nc, L, h_).transpose(1, 0, 3, 2)
    lfc = log_f.reshape(bsz, nc, L, h_).transpose(1, 0, 3, 2)
    causal = jnp.tril(jnp.ones((L, L), dtype=bool))

    def step(carry, xs):
        C, n, m = carry
        qb, kb, vb, li, lf = xs
        b = jnp.cumsum(lf, axis=-1)
        logD = b[..., :, None] - b[..., None, :] + li[..., None, :]
        logD = jnp.where(causal, logD, -jnp.inf)
        inter = b + m[..., None]
        m_t = jnp.maximum(inter, jnp.max(logD, axis=-1))
        Dm = jnp.exp(logD - m_t[..., None])
        sc = jnp.exp(inter - m_t)
        sk = jnp.einsum('bhtd,bhsd->bhts', qb, kb) * Dm
        num = jnp.einsum('bhts,bhsv->bhtv', sk, vb) + sc[..., None] * jnp.einsum('bhtd,bhdv->bhtv', qb, C)
        den = jnp.sum(sk, axis=-1) + sc * jnp.einsum('bhtd,bhd->bht', qb, n)
        hb = num / jnp.maximum(jnp.abs(den), jnp.exp(-m_t))[..., None]
        bL = b[..., -1]
        logw = bL[..., None] - b + li
        m_new = jnp.maximum(bL + m, jnp.max(logw, axis=-1))
        w = jnp.exp(logw - m_new[..., None])
        decay = jnp.exp(bL + m - m_new)
        C_new = decay[..., None, None] * C + jnp.einsum('bhs,bhsd,bhsv->bhdv', w, kb, vb)
        n_new = decay[..., None] * n + jnp.einsum('bhs,bhsd->bhd', w, kb)
        return (C_new, n_new, m_new), hb

    init = (jnp.zeros((bsz, h_, dqk, dv), jnp.float32),
            jnp.zeros((bsz, h_, dqk), jnp.float32),
            jnp.zeros((bsz, h_), jnp.float32))
    _, hs = lax.scan(step, init, (qc, kc, vc, lic, lfc))
    hs = hs.transpose(1, 0, 3, 2, 4).reshape(bsz, s, h_, dv)
    hs = hs * lax.rsqrt(jnp.mean(hs * hs, axis=-1, keepdims=True) + EPS) * head_norm.astype(jnp.float32)
    hs = hs.reshape(bsz, s, MLSTM_V) * o_gate
    return hs.astype(x.dtype) @ w_out


def short_conv_mixer(x, w_in, conv_w, conv_b, w_out):
    proj = x @ w_in
    bgate = proj[..., :SCONV_WIDTH]
    cgate = proj[..., SCONV_WIDTH:2 * SCONV_WIDTH]
    xin = proj[..., 2 * SCONV_WIDTH:]
    y = causal_dwconv(cgate * xin, conv_w, conv_b)
    return (bgate * y) @ w_out


def conv_ffn(x, w_up, conv_w, conv_b, w_down):
    u = causal_dwconv(x @ w_up, conv_w, conv_b)
    gate = u[..., :D_FF]
    val = u[..., D_FF:]
    return (jax.nn.silu(gate) * val) @ w_down


def setup_inputs(seed: int = 0) -> dict:
    key = jax.random.key(seed)
    ks = jax.random.split(key, 24)
    f32 = jnp.float32

    def nrm(k, shape, scale):
        return jax.random.normal(k, shape, f32) * scale

    def gain(k, shape):
        return 1.0 + 0.02 * jax.random.normal(k, shape, f32)

    b_i = 0.1 * jax.random.normal(ks[3], (MLSTM_HEADS,), f32)
    b_f = 3.0 + 0.5 * jax.random.normal(ks[4], (MLSTM_HEADS,), f32)
    return {
        "x": jax.random.normal(ks[0], (BATCH, SEQ, D_MODEL), f32),
        "l0_norm_mix": gain(ks[1], (D_MODEL,)),
        "l0_mlstm_w_in": nrm(ks[2], (D_MODEL, MLSTM_IN), D_MODEL ** -0.5),
        "l0_mlstm_b_gates": jnp.concatenate([b_i, b_f]),
        "l0_mlstm_head_norm": gain(ks[5], (MLSTM_HEADS, MLSTM_DV)),
        "l0_mlstm_w_out": nrm(ks[6], (MLSTM_V, D_MODEL), MLSTM_V ** -0.5),
        "l0_norm_ffn": gain(ks[7], (D_MODEL,)),
        "l0_ffn_w_up": nrm(ks[8], (D_MODEL, 2 * D_FF), D_MODEL ** -0.5),
        "l0_ffn_conv_w": nrm(ks[9], (CONV_K, 2 * D_FF), CONV_K ** -0.5),
        "l0_ffn_conv_b": nrm(ks[10], (2 * D_FF,), 0.02),
        "l0_ffn_w_down": nrm(ks[11], (D_FF, D_MODEL), D_FF ** -0.5),
        "l1_norm_mix": gain(ks[12], (D_MODEL,)),
        "l1_sconv_w_in": nrm(ks[13], (D_MODEL, 3 * SCONV_WIDTH), D_MODEL ** -0.5),
        "l1_sconv_conv_w": nrm(ks[14], (CONV_K, SCONV_WIDTH), CONV_K ** -0.5),
        "l1_sconv_conv_b": nrm(ks[15], (SCONV_WIDTH,), 0.02),
        "l1_sconv_w_out": nrm(ks[16], (SCONV_WIDTH, D_MODEL), SCONV_WIDTH ** -0.5),
        "l1_norm_ffn": gain(ks[17], (D_MODEL,)),
        "l1_ffn_w_up": nrm(ks[18], (D_MODEL, 2 * D_FF), D_MODEL ** -0.5),
        "l1_ffn_conv_w": nrm(ks[19], (CONV_K, 2 * D_FF), CONV_K ** -0.5),
        "l1_ffn_conv_b": nrm(ks[20], (2 * D_FF,), 0.02),
        "l1_ffn_w_down": nrm(ks[21], (D_FF, D_MODEL), D_FF ** -0.5),
        "final_norm": gain(ks[22], (D_MODEL,)),
    }


def reference(x, l0_norm_mix, l0_mlstm_w_in, l0_mlstm_b_gates, l0_mlstm_head_norm, l0_mlstm_w_out,
              l0_norm_ffn, l0_ffn_w_up, l0_ffn_conv_w, l0_ffn_conv_b, l0_ffn_w_down,
              l1_norm_mix, l1_sconv_w_in, l1_sconv_conv_w, l1_sconv_conv_b, l1_sconv_w_out,
              l1_norm_ffn, l1_ffn_w_up, l1_ffn_conv_w, l1_ffn_conv_b, l1_ffn_w_down,
              final_norm):
    mix_norms = [l0_norm_mix, l1_norm_mix]
    mix_params = [(l0_mlstm_w_in, l0_mlstm_b_gates, l0_mlstm_head_norm, l0_mlstm_w_out),
                  (l1_sconv_w_in, l1_sconv_conv_w, l1_sconv_conv_b, l1_sconv_w_out)]
    ffn_norms = [l0_norm_ffn, l1_norm_ffn]
    ffn_params = [(l0_ffn_w_up, l0_ffn_conv_w, l0_ffn_conv_b, l0_ffn_w_down),
                  (l1_ffn_w_up, l1_ffn_conv_w, l1_ffn_conv_b, l1_ffn_w_down)]
    mixers = [mlstm_mixer, short_conv_mixer]
    for i in range(DEPTH):
        mixer = mixers[i % N_MIXERS]
        x = x + mixer(rmsnorm(x, mix_norms[i]), *mix_params[i])
        x = x + conv_ffn(rmsnorm(x, ffn_norms[i]), *ffn_params[i])
    return rmsnorm(x, final_norm)
```

```python
import functools

import jax
import jax.numpy as jnp
from jax import lax
from jax.experimental import pallas as pl
from jax.experimental.pallas import tpu as pltpu

EPS = 1e-6
CONV_K = 3

V7X_LANES = 128
V7X_SUBLANES = 8
V7X_MXU_DIM = 256
V7X_VMEM_LIMIT_BYTES = 56 * 1024 * 1024

TOKEN_TILE = 512
MLSTM_CHUNK = 256
CONV_COLS = 256


def _rms(x, g):
    ms = jnp.mean(x * x, axis=-1, keepdims=True)
    return x * lax.rsqrt(ms + EPS) * g


def _dot(a, b):
    return jnp.dot(a, b, preferred_element_type=jnp.float32)


def _shift_rows(u, prev, s):
    r = pltpu.roll(u, s, 0)
    p = pltpu.roll(prev, s, 0)
    rid = lax.broadcasted_iota(jnp.int32, p.shape, 0)
    head = jnp.where(rid < s, p, r[0:V7X_SUBLANES])
    return jnp.concatenate([head, r[V7X_SUBLANES:]], axis=0)


def _causal_conv3(u, prev, cw, cb):
    u1 = _shift_rows(u, prev, 1)
    u2 = _shift_rows(u, prev, 2)
    return cw[0:1] * u2 + cw[1:2] * u1 + cw[2:3] * u + cb


def _conv_block_kernel(x_ref, g_ref, wu_ref, cw_ref, cb_ref, wd_ref, gf_ref, o_ref,
                       xn_ref, act_ref, carry_ref, *, kind, width, tiles_per_seq, final_norm):
    tm = x_ref.shape[0]

    @pl.when(pl.program_id(0) % tiles_per_seq == 0)
    def _():
        carry_ref[...] = jnp.zeros_like(carry_ref)

    x = x_ref[...]
    xn_ref[...] = _rms(x, g_ref[...]).astype(jnp.bfloat16)

    for c in range(width // CONV_COLS):
        lo = c * CONV_COLS
        cols = slice(lo, lo + CONV_COLS)
        xn = xn_ref[...]
        if kind == "ffn":
            outs = []
            for part in range(2):
                pc = slice(part * width + lo, part * width + lo + CONV_COLS)
                u = _dot(xn, wu_ref[:, pc])
                prev = carry_ref[:, pc]
                carry_ref[:, pc] = u[tm - V7X_SUBLANES:]
                outs.append(_causal_conv3(u, prev, cw_ref[:, pc], cb_ref[:, pc]))
            act = jax.nn.silu(outs[0]) * outs[1]
        else:
            bg = _dot(xn, wu_ref[:, cols])
            cg = _dot(xn, wu_ref[:, width + lo:width + lo + CONV_COLS])
            xi = _dot(xn, wu_ref[:, 2 * width + lo:2 * width + lo + CONV_COLS])
            u = cg * xi
            prev = carry_ref[:, cols]
            carry_ref[:, cols] = u[tm - V7X_SUBLANES:]
            act = bg * _causal_conv3(u, prev, cw_ref[:, cols], cb_ref[:, cols])
        act_ref[:, cols] = act.astype(jnp.bfloat16)

    y = x + _dot(act_ref[...], wd_ref[...])
    if final_norm:
        y = _rms(y, gf_ref[...])
    o_ref[...] = y


def _conv_block(x, g, w_up, conv_w, conv_b, w_down, g_final, *, kind, seq_len, final_norm):
    t, d = x.shape
    width = w_down.shape[0]
    n_conv = conv_w.shape[1]
    tm = TOKEN_TILE
    const = lambda i: (0, 0)
    once = pl.Buffered(1)
    kern = functools.partial(_conv_block_kernel, kind=kind, width=width,
                             tiles_per_seq=seq_len // tm, final_norm=final_norm)
    return pl.pallas_call(
        kern,
        grid=(t // tm,),
        in_specs=[
            pl.BlockSpec((tm, d), lambda i: (i, 0)),
            pl.BlockSpec((1, d), const, pipeline_mode=once),
            pl.BlockSpec(w_up.shape, const, pipeline_mode=once),
            pl.BlockSpec(conv_w.shape, const, pipeline_mode=once),
            pl.BlockSpec((1, n_conv), const, pipeline_mode=once),
            pl.BlockSpec(w_down.shape, const, pipeline_mode=once),
            pl.BlockSpec((1, d), const, pipeline_mode=once),
        ],
        out_specs=pl.BlockSpec((tm, d), lambda i: (i, 0)),
        out_shape=jax.ShapeDtypeStruct((t, d), jnp.float32),
        scratch_shapes=[
            pltpu.VMEM((tm, d), jnp.bfloat16),
            pltpu.VMEM((tm, width), jnp.bfloat16),
            pltpu.VMEM((V7X_SUBLANES, n_conv), jnp.float32),
        ],
        compiler_params=pltpu.CompilerParams(
            dimension_semantics=("arbitrary",), vmem_limit_bytes=V7X_VMEM_LIMIT_BYTES),
        name=f"{kind}_block",
    )(x, g.reshape(1, d), w_up.astype(jnp.bfloat16), conv_w, conv_b.reshape(1, n_conv),
      w_down.astype(jnp.bfloat16), g_final.reshape(1, d))


def _cumsum_rows(x):
    n = x.shape[0]
    rid = lax.broadcasted_iota(jnp.int32, x.shape, 0)
    s = 1
    while s < n:
        x = x + jnp.where(rid >= s, pltpu.roll(x, s, 0), 0.0)
        s *= 2
    return x


def _mlstm_kernel(x_ref, g_ref, win_ref, wg_ref, bg_ref, hn_ref, wo_ref, o_ref,
                  xn_ref, q_ref, k_ref, v_ref, og_ref, gc_ref, gr_ref, hs_ref, hb_ref,
                  c_ref, m_ref, *, heads, dqk, dv, tiles_per_seq):
    tm = x_ref.shape[0]
    L = MLSTM_CHUNK
    qk = heads * dqk
    vw = heads * dv
    dve = dv + V7X_LANES

    @pl.when(pl.program_id(0) % tiles_per_seq == 0)
    def _():
        c_ref[...] = jnp.zeros_like(c_ref)
        m_ref[...] = jnp.zeros_like(m_ref)

    x = x_ref[...]
    xn_ref[...] = _rms(x, g_ref[...]).astype(jnp.bfloat16)
    xn = xn_ref[...]

    q_ref[...] = (_dot(xn, win_ref[:, 0:qk]) * (dqk ** -0.5)).astype(jnp.bfloat16)
    k_ref[...] = _dot(xn, win_ref[:, qk:2 * qk]).astype(jnp.bfloat16)
    ones_col = (lax.broadcasted_iota(jnp.int32, (tm, V7X_LANES), 1) == 0).astype(jnp.bfloat16)
    for h in range(heads):
        v_ref[h, :, 0:dv] = _dot(xn, win_ref[:, 2 * qk + h * dv:2 * qk + (h + 1) * dv]).astype(jnp.bfloat16)
        v_ref[h, :, dv:dve] = ones_col
    og_ref[...] = _dot(xn, win_ref[:, 2 * qk + vw:2 * qk + 2 * vw])

    gl = _dot(xn, wg_ref[...]) + bg_ref[...]
    lf = jax.nn.log_sigmoid(gl)
    b = jnp.concatenate([_cumsum_rows(lf[c * L:(c + 1) * L]) for c in range(tm // L)], axis=0)
    a = gl - pltpu.roll(b, V7X_LANES - heads, 1)
    lane = lax.broadcasted_iota(jnp.int32, (tm, V7X_LANES), 1)
    gcols = jnp.where(lane < heads, a, b)
    gc_ref[...] = gcols
    gr_ref[...] = gcols.T

    causal = (lax.broadcasted_iota(jnp.int32, (L, L), 0) >= lax.broadcasted_iota(jnp.int32, (L, L), 1))

    for c in range(tm // L):
        rows = slice(c * L, (c + 1) * L)
        for h in range(heads):
            qh = q_ref[rows, h * dqk:(h + 1) * dqk]
            kh = k_ref[rows, h * dqk:(h + 1) * dqk]
            ve = v_ref[h, rows, :]
            a_col = gc_ref[rows, h:h + 1]
            b_col = gc_ref[rows, heads + h:heads + h + 1]
            a_row = gr_ref[h:h + 1, rows]
            b_last = gr_ref[heads + h:heads + h + 1, (c + 1) * L - 1:(c + 1) * L]
            m_prev = m_ref[h, 0:1, 0:1]
            cst = c_ref[h]

            log_d = jnp.where(causal, b_col + a_row, -jnp.inf)
            inter = b_col + m_prev
            m_t = jnp.maximum(inter, jnp.max(log_d, axis=-1, keepdims=True))
            dm = jnp.exp(log_d - m_t)
            sc = jnp.exp(inter - m_t)
            sk = lax.dot_general(qh, kh, (((1,), (1,)), ((), ())),
                                 preferred_element_type=jnp.float32) * dm
            tot = _dot(sk.astype(jnp.bfloat16), ve) + sc * _dot(qh, cst.astype(jnp.bfloat16))
            den = tot[:, dv:dv + 1]
            inv = 1.0 / jnp.maximum(jnp.abs(den), jnp.exp(-m_t))
            hs_ref[rows, h * dv:(h + 1) * dv] = tot[:, 0:dv] * inv

            m_new = b_last + jnp.maximum(m_prev, jnp.max(a_row, axis=-1, keepdims=True))
            w_col = jnp.exp(b_last + a_col - m_new)
            decay = jnp.exp(b_last + m_prev - m_new)
            kw = (kh.astype(jnp.float32) * w_col).astype(jnp.bfloat16)
            upd = lax.dot_general(kw, ve, (((0,), (0,)), ((), ())),
                                  preferred_element_type=jnp.float32)
            c_ref[h] = decay * cst + upd
            m_ref[h] = jnp.broadcast_to(m_new, m_ref.shape[1:])

    for h in range(heads):
        hc = slice(h * dv, (h + 1) * dv)
        hs = hs_ref[:, hc]
        hn = _rms(hs, hn_ref[:, hc]) * jax.nn.sigmoid(og_ref[:, hc])
        hb_ref[:, hc] = hn.astype(jnp.bfloat16)
    o_ref[...] = x + _dot(hb_ref[...], wo_ref[...])


def _mlstm_block(x, g, w_in, b_gates, head_norm, w_out, *, seq_len):
    t, d = x.shape
    heads, dv = head_norm.shape
    vw = heads * dv
    qk = (w_in.shape[1] - 2 * vw - 2 * heads) // 2
    dqk = qk // heads
    tm = TOKEN_TILE
    n_main = 2 * qk + 2 * vw
    w_main = w_in[:, :n_main].astype(jnp.bfloat16)
    w_gate = jnp.pad(w_in[:, n_main:], ((0, 0), (0, V7X_LANES - 2 * heads))).astype(jnp.bfloat16)
    b_gate = jnp.pad(b_gates, (0, V7X_LANES - 2 * heads)).reshape(1, V7X_LANES)
    const = lambda i: (0, 0)
    once = pl.Buffered(1)
    kern = functools.partial(_mlstm_kernel, heads=heads, dqk=dqk, dv=dv, tiles_per_seq=seq_len // tm)
    return pl.pallas_call(
        kern,
        grid=(t // tm,),
        in_specs=[
            pl.BlockSpec((tm, d), lambda i: (i, 0)),
            pl.BlockSpec((1, d), const, pipeline_mode=once),
            pl.BlockSpec(w_main.shape, const, pipeline_mode=once),
            pl.BlockSpec(w_gate.shape, const, pipeline_mode=once),
            pl.BlockSpec((1, V7X_LANES), const, pipeline_mode=once),
            pl.BlockSpec((1, vw), const, pipeline_mode=once),
            pl.BlockSpec(w_out.shape, const, pipeline_mode=once),
        ],
        out_specs=pl.BlockSpec((tm, d), lambda i: (i, 0)),
        out_shape=jax.ShapeDtypeStruct((t, d), jnp.float32),
        scratch_shapes=[
            pltpu.VMEM((tm, d), jnp.bfloat16),
            pltpu.VMEM((tm, qk), jnp.bfloat16),
            pltpu.VMEM((tm, qk), jnp.bfloat16),
            pltpu.VMEM((heads, tm, dv + V7X_LANES), jnp.bfloat16),
            pltpu.VMEM((tm, vw), jnp.float32),
            pltpu.VMEM((tm, V7X_LANES), jnp.float32),
            pltpu.VMEM((V7X_LANES, tm), jnp.float32),
            pltpu.VMEM((tm, vw), jnp.float32),
            pltpu.VMEM((tm, vw), jnp.bfloat16),
            pltpu.VMEM((heads, dqk, dv + V7X_LANES), jnp.float32),
            pltpu.VMEM((heads, V7X_SUBLANES, V7X_LANES), jnp.float32),
        ],
        compiler_params=pltpu.CompilerParams(
            dimension_semantics=("arbitrary",), vmem_limit_bytes=V7X_VMEM_LIMIT_BYTES),
        name="mlstm_block",
    )(x, g.reshape(1, d), w_main, w_gate, b_gate, head_norm.reshape(1, vw), w_out.astype(jnp.bfloat16))


def kernel(x, l0_norm_mix, l0_mlstm_w_in, l0_mlstm_b_gates, l0_mlstm_head_norm, l0_mlstm_w_out, l0_norm_ffn, l0_ffn_w_up, l0_ffn_conv_w, l0_ffn_conv_b, l0_ffn_w_down, l1_norm_mix, l1_sconv_w_in, l1_sconv_conv_w, l1_sconv_conv_b, l1_sconv_w_out, l1_norm_ffn, l1_ffn_w_up, l1_ffn_conv_w, l1_ffn_conv_b, l1_ffn_w_down, final_norm):
    bsz, seq_len, d = x.shape
    h = x.reshape(bsz * seq_len, d)
    h = _mlstm_block(h, l0_norm_mix, l0_mlstm_w_in, l0_mlstm_b_gates, l0_mlstm_head_norm,
                     l0_mlstm_w_out, seq_len=seq_len)
    h = _conv_block(h, l0_norm_ffn, l0_ffn_w_up, l0_ffn_conv_w, l0_ffn_conv_b, l0_ffn_w_down,
                    final_norm, kind="ffn", seq_len=seq_len, final_norm=False)
    h = _conv_block(h, l1_norm_mix, l1_sconv_w_in, l1_sconv_conv_w, l1_sconv_conv_b, l1_sconv_w_out,
                    final_norm, kind="sconv", seq_len=seq_len, final_norm=False)
    h = _conv_block(h, l1_norm_ffn, l1_ffn_w_up, l1_ffn_conv_w, l1_ffn_conv_b, l1_ffn_w_down,
                    final_norm, kind="ffn", seq_len=seq_len, final_norm=True)
    return h.reshape(bsz, seq_len, d)
```

```python
import functools

import jax
import jax.numpy as jnp
from jax import lax
from jax.experimental import pallas as pl
from jax.experimental.pallas import tpu as pltpu

EPS = 1e-6
CONV_K = 3

V7X_LANES = 128
V7X_SUBLANES = 8
V7X_MXU_DIM = 256
V7X_VMEM_LIMIT_BYTES = 56 * 1024 * 1024

TOKEN_TILE = 512
MLSTM_CHUNK = 256


def _rms(x, g):
    ms = jnp.mean(x * x, axis=-1, keepdims=True)
    return x * lax.rsqrt(ms + EPS) * g


def _dot(a, b):
    return jnp.dot(a, b, preferred_element_type=jnp.float32)


def _slab_pitch(tm):
    return tm // V7X_SUBLANES + V7X_SUBLANES


def _tile_to_runs(x_ref, slab_ref, dst_ref):
    tm, d = x_ref.shape
    p, pitch = tm // V7X_SUBLANES, _slab_pitch(tm)
    for k in range(d // V7X_LANES):
        lanes = slice(k * V7X_LANES, (k + 1) * V7X_LANES)
        for s in range(V7X_SUBLANES):
            slab_ref[k, s * pitch:s * pitch + p, :] = x_ref[s * p:(s + 1) * p, lanes]
        for j in range(p):
            dst_ref[j * V7X_SUBLANES:(j + 1) * V7X_SUBLANES, lanes] = (
                slab_ref[k, pl.ds(j, V7X_SUBLANES, stride=pitch), :])


def _tile_from_runs(y, slab_ref, o_ref):
    tm, d = y.shape
    p, pitch = tm // V7X_SUBLANES, _slab_pitch(tm)
    for k in range(d // V7X_LANES):
        lanes = slice(k * V7X_LANES, (k + 1) * V7X_LANES)
        for j in range(p):
            slab_ref[k, pl.ds(j, V7X_SUBLANES, stride=pitch), :] = (
                y[j * V7X_SUBLANES:(j + 1) * V7X_SUBLANES, lanes])
        for s in range(V7X_SUBLANES):
            o_ref[s * p:(s + 1) * p, lanes] = slab_ref[k, s * pitch:s * pitch + p, :]


def _causal_conv3(u, prev, cw, cb):
    tm = u.shape[0]
    s8 = V7X_SUBLANES
    first = lax.broadcasted_iota(jnp.int32, (s8, u.shape[1]), 0) == 0

    def run_before(cur_blk, prev_blk):
        return jnp.where(first, pltpu.roll(prev_blk, 1, 0), pltpu.roll(cur_blk, 1, 0))

    h1 = run_before(u[tm - s8:], prev[s8:])
    h2 = run_before(u[tm - 2 * s8:tm - s8], prev[:s8])
    u1 = jnp.concatenate([h1, u[:tm - s8]], axis=0)
    u2 = jnp.concatenate([h2, h1, u[:tm - 2 * s8]], axis=0)
    return cw[0:1] * u2 + cw[1:2] * u1 + cw[2:3] * u + cb


CONV_COLS = V7X_MXU_DIM


def _conv_block_kernel(x_ref, g_ref, wu_ref, cw_ref, cb_ref, wd_ref, gf_ref, o_ref,
                       xn_ref, act_ref, carry_ref, slab_ref, xr_ref, *,
                       kind, tiles_per_seq, final_norm, runs_in, runs_out):
    tm = x_ref.shape[0]
    width = wd_ref.shape[0]
    cc = CONV_COLS

    def conv(u, ccols):
        prev = carry_ref[:, ccols]
        carry_ref[:, ccols] = u[tm - 2 * V7X_SUBLANES:]
        return _causal_conv3(u, prev, cw_ref[:, ccols], cb_ref[:, ccols])

    @pl.when(pl.program_id(0) % tiles_per_seq == 0)
    def _():
        carry_ref[...] = jnp.zeros_like(carry_ref)

    if runs_in:
        x = x_ref[...]
    else:
        _tile_to_runs(x_ref, slab_ref, xr_ref)
        x = xr_ref[...]
    xn_ref[...] = _rms(x, g_ref[...]).astype(jnp.bfloat16)

    for c in range(width // cc):
        xn = xn_ref[...]
        cols = [slice(grp * width + c * cc, grp * width + (c + 1) * cc) for grp in range(3)]
        if kind == "ffn":
            gate, val = [conv(_dot(xn, wu_ref[:, cols[grp]]), cols[grp]) for grp in range(2)]
            act = jax.nn.silu(gate) * val
        else:
            bg, cg, xi = [_dot(xn, wu_ref[:, cols[grp]]) for grp in range(3)]
            act = bg * conv(cg * xi, cols[0])
        act_ref[:, cols[0]] = act.astype(jnp.bfloat16)

    y = x + _dot(act_ref[...], wd_ref[...])
    if final_norm:
        y = _rms(y, gf_ref[...])
    if runs_out:
        o_ref[...] = y
    else:
        _tile_from_runs(y, slab_ref, o_ref)


def _conv_block(x, g, w_up, conv_w, conv_b, w_down, g_final, *, kind, seq_len, final_norm,
                runs_in, runs_out):
    t, d = x.shape
    width = w_down.shape[0]
    n_conv = conv_w.shape[1]
    tm = TOKEN_TILE
    w_up = w_up.astype(jnp.bfloat16)
    const = lambda i: (0, 0)
    once = pl.Buffered(1)
    kern = functools.partial(_conv_block_kernel, kind=kind, tiles_per_seq=seq_len // tm,
                             final_norm=final_norm, runs_in=runs_in, runs_out=runs_out)
    return pl.pallas_call(
        kern,
        grid=(t // tm,),
        in_specs=[
            pl.BlockSpec((tm, d), lambda i: (i, 0)),
            pl.BlockSpec((1, d), const, pipeline_mode=once),
            pl.BlockSpec(w_up.shape, const, pipeline_mode=once),
            pl.BlockSpec(conv_w.shape, const, pipeline_mode=once),
            pl.BlockSpec((1, n_conv), const, pipeline_mode=once),
            pl.BlockSpec(w_down.shape, const, pipeline_mode=once),
            pl.BlockSpec((1, d), const, pipeline_mode=once),
        ],
        out_specs=pl.BlockSpec((tm, d), lambda i: (i, 0)),
        out_shape=jax.ShapeDtypeStruct((t, d), jnp.float32),
        scratch_shapes=[
            pltpu.VMEM((tm, d), jnp.bfloat16),
            pltpu.VMEM((tm, width), jnp.bfloat16),
            pltpu.VMEM((2 * V7X_SUBLANES, n_conv), jnp.float32),
            pltpu.VMEM((d // V7X_LANES, V7X_SUBLANES * _slab_pitch(tm), V7X_LANES),
                       jnp.float32),
            pltpu.VMEM((tm, d), jnp.float32),
        ],
        compiler_params=pltpu.CompilerParams(
            dimension_semantics=("arbitrary",), vmem_limit_bytes=V7X_VMEM_LIMIT_BYTES),
        name=f"{kind}_block",
    )(x, g.reshape(1, d), w_up, conv_w, conv_b.reshape(1, n_conv),
      w_down.astype(jnp.bfloat16), g_final.reshape(1, d))


def _cumsum_rows(x):
    n = x.shape[0]
    rid = lax.broadcasted_iota(jnp.int32, x.shape, 0)
    s = 1
    while s < n:
        x = x + jnp.where(rid >= s, pltpu.roll(x, s, 0), 0.0)
        s *= 2
    return x


def _mlstm_kernel(x_ref, g_ref, win_ref, wg_ref, bg_ref, hn_ref, wo_ref, o_ref,
                  xn_ref, q_ref, k_ref, v_ref, og_ref, gc_ref, gr_ref, hs_ref, hb_ref,
                  c_ref, n_ref, m_ref, *, heads, dqk, dv, tiles_per_seq):
    tm = x_ref.shape[0]
    L = MLSTM_CHUNK
    qk = heads * dqk
    vw = heads * dv

    @pl.when(pl.program_id(0) % tiles_per_seq == 0)
    def _():
        c_ref[...] = jnp.zeros_like(c_ref)
        n_ref[...] = jnp.zeros_like(n_ref)
        m_ref[...] = jnp.zeros_like(m_ref)

    x = x_ref[...]
    xn_ref[...] = _rms(x, g_ref[...]).astype(jnp.bfloat16)
    xn = xn_ref[...]

    gl = _dot(xn, wg_ref[...]) + bg_ref[...]
    lf = jax.nn.log_sigmoid(gl)
    b = jnp.concatenate([_cumsum_rows(lf[c * L:(c + 1) * L]) for c in range(tm // L)], axis=0)
    a = gl - pltpu.roll(b, V7X_LANES - heads, 1)
    lane = lax.broadcasted_iota(jnp.int32, (tm, V7X_LANES), 1)
    gcols = jnp.where(lane < heads, a, b)
    gc_ref[...] = gcols
    gr_ref[...] = gcols.T

    q_ref[...] = (_dot(xn, win_ref[:, 0:qk]) * (dqk ** -0.5)).astype(jnp.bfloat16)
    k_ref[...] = _dot(xn, win_ref[:, qk:2 * qk]).astype(jnp.bfloat16)
    v_ref[...] = _dot(xn, win_ref[:, 2 * qk:2 * qk + vw]).astype(jnp.bfloat16)
    og_ref[...] = _dot(xn, win_ref[:, 2 * qk + vw:2 * qk + 2 * vw])

    causal = (lax.broadcasted_iota(jnp.int32, (L, L), 0) >= lax.broadcasted_iota(jnp.int32, (L, L), 1))

    for c in range(tm // L):
        rows = slice(c * L, (c + 1) * L)
        for h in range(heads):
            qh = q_ref[rows, h * dqk:(h + 1) * dqk]
            kh = k_ref[rows, h * dqk:(h + 1) * dqk]
            vh = v_ref[rows, h * dv:(h + 1) * dv]
            a_col = gc_ref[rows, h:h + 1]
            b_col = gc_ref[rows, heads + h:heads + h + 1]
            a_row = gr_ref[h:h + 1, rows]
            b_last = gr_ref[heads + h:heads + h + 1, (c + 1) * L - 1:(c + 1) * L]
            m_prev = m_ref[h, 0:1, 0:1]
            cst = c_ref[h]
            nst = n_ref[h]

            log_d = jnp.where(causal, b_col + a_row, -jnp.inf)
            inter = b_col + m_prev
            m_t = jnp.maximum(inter, jnp.max(log_d, axis=-1, keepdims=True))
            dm = jnp.exp(log_d - m_t)
            sc = jnp.exp(inter - m_t)
            sk = lax.dot_general(qh, kh, (((1,), (1,)), ((), ())),
                                 preferred_element_type=jnp.float32) * dm
            num = _dot(sk.astype(jnp.bfloat16), vh) + sc * _dot(qh, cst.astype(jnp.bfloat16))
            qn = jnp.sum(qh.astype(jnp.float32) * nst, axis=-1, keepdims=True)
            den = jnp.sum(sk, axis=-1, keepdims=True) + sc * qn
            inv = 1.0 / jnp.maximum(jnp.abs(den), jnp.exp(-m_t))
            hs_ref[rows, h * dv:(h + 1) * dv] = num * inv

            m_new = b_last + jnp.maximum(m_prev, jnp.max(a_row, axis=-1, keepdims=True))
            w_col = jnp.exp(b_last + a_col - m_new)
            decay = jnp.exp(b_last + m_prev - m_new)
            kw = kh.astype(jnp.float32) * w_col
            upd = lax.dot_general(kw.astype(jnp.bfloat16), vh, (((0,), (0,)), ((), ())),
                                  preferred_element_type=jnp.float32)
            c_ref[h] = decay * cst + upd
            n_ref[h] = decay * nst + jnp.sum(kw, axis=0, keepdims=True)
            m_ref[h] = jnp.broadcast_to(m_new, m_ref.shape[1:])

    for h in range(heads):
        hc = slice(h * dv, (h + 1) * dv)
        hs = hs_ref[:, hc]
        hn = _rms(hs, hn_ref[:, hc]) * jax.nn.sigmoid(og_ref[:, hc])
        hb_ref[:, hc] = hn.astype(jnp.bfloat16)
    o_ref[...] = x + _dot(hb_ref[...], wo_ref[...])


def _mlstm_block(x, g, w_in, b_gates, head_norm, w_out, *, seq_len):
    t, d = x.shape
    heads, dv = head_norm.shape
    vw = heads * dv
    qk = (w_in.shape[1] - 2 * vw - 2 * heads) // 2
    dqk = qk // heads
    tm = TOKEN_TILE
    n_main = 2 * qk + 2 * vw
    w_main = w_in[:, :n_main].astype(jnp.bfloat16)
    w_gate = jnp.pad(w_in[:, n_main:], ((0, 0), (0, V7X_LANES - 2 * heads))).astype(jnp.bfloat16)
    b_gate = jnp.pad(b_gates, (0, V7X_LANES - 2 * heads)).reshape(1, V7X_LANES)
    const = lambda i: (0, 0)
    once = pl.Buffered(1)
    kern = functools.partial(_mlstm_kernel, heads=heads, dqk=dqk, dv=dv, tiles_per_seq=seq_len // tm)
    return pl.pallas_call(
        kern,
        grid=(t // tm,),
        in_specs=[
            pl.BlockSpec((tm, d), lambda i: (i, 0)),
            pl.BlockSpec((1, d), const, pipeline_mode=once),
            pl.BlockSpec(w_main.shape, const, pipeline_mode=once),
            pl.BlockSpec(w_gate.shape, const, pipeline_mode=once),
            pl.BlockSpec((1, V7X_LANES), const, pipeline_mode=once),
            pl.BlockSpec((1, vw), const, pipeline_mode=once),
            pl.BlockSpec(w_out.shape, const, pipeline_mode=once),
        ],
        out_specs=pl.BlockSpec((tm, d), lambda i: (i, 0)),
        out_shape=jax.ShapeDtypeStruct((t, d), jnp.float32),
        scratch_shapes=[
            pltpu.VMEM((tm, d), jnp.bfloat16),
            pltpu.VMEM((tm, qk), jnp.bfloat16),
            pltpu.VMEM((tm, qk), jnp.bfloat16),
            pltpu.VMEM((tm, vw), jnp.bfloat16),
            pltpu.VMEM((tm, vw), jnp.float32),
            pltpu.VMEM((tm, V7X_LANES), jnp.float32),
            pltpu.VMEM((V7X_LANES, tm), jnp.float32),
            pltpu.VMEM((tm, vw), jnp.float32),
            pltpu.VMEM((tm, vw), jnp.bfloat16),
            pltpu.VMEM((heads, dqk, dv), jnp.float32),
            pltpu.VMEM((heads, 1, dqk), jnp.float32),
            pltpu.VMEM((heads, V7X_SUBLANES, V7X_LANES), jnp.float32),
        ],
        compiler_params=pltpu.CompilerParams(
            dimension_semantics=("arbitrary",), vmem_limit_bytes=V7X_VMEM_LIMIT_BYTES),
        name="mlstm_block",
    )(x, g.reshape(1, d), w_main, w_gate, b_gate, head_norm.reshape(1, vw), w_out.astype(jnp.bfloat16))


def kernel(x, l0_norm_mix, l0_mlstm_w_in, l0_mlstm_b_gates, l0_mlstm_head_norm, l0_mlstm_w_out, l0_norm_ffn, l0_ffn_w_up, l0_ffn_conv_w, l0_ffn_conv_b, l0_ffn_w_down, l1_norm_mix, l1_sconv_w_in, l1_sconv_conv_w, l1_sconv_conv_b, l1_sconv_w_out, l1_norm_ffn, l1_ffn_w_up, l1_ffn_conv_w, l1_ffn_conv_b, l1_ffn_w_down, final_norm):
    bsz, seq_len, d = x.shape
    h = x.reshape(bsz * seq_len, d)
    h = _mlstm_block(h, l0_norm_mix, l0_mlstm_w_in, l0_mlstm_b_gates, l0_mlstm_head_norm,
                     l0_mlstm_w_out, seq_len=seq_len)
    h = _conv_block(h, l0_norm_ffn, l0_ffn_w_up, l0_ffn_conv_w, l0_ffn_conv_b, l0_ffn_w_down,
                    final_norm, kind="ffn", seq_len=seq_len, final_norm=False,
                    runs_in=False, runs_out=True)
    h = _conv_block(h, l1_norm_mix, l1_sconv_w_in, l1_sconv_conv_w, l1_sconv_conv_b, l1_sconv_w_out,
                    final_norm, kind="sconv", seq_len=seq_len, final_norm=False,
                    runs_in=True, runs_out=True)
    h = _conv_block(h, l1_norm_ffn, l1_ffn_w_up, l1_ffn_conv_w, l1_ffn_conv_b, l1_ffn_w_down,
                    final_norm, kind="ffn", seq_len=seq_len, final_norm=True,
                    runs_in=True, runs_out=False)
    return h.reshape(bsz, seq_len, d)
```

```python
import functools

import jax
import jax.numpy as jnp
from jax import lax
from jax.experimental import pallas as pl
from jax.experimental.pallas import tpu as pltpu

EPS = 1e-6
CONV_K = 3

V7X_LANES = 128
V7X_SUBLANES = 8
V7X_MXU_DIM = 256
V7X_VMEM_LIMIT_BYTES = 56 * 1024 * 1024

TOKEN_TILE = 512
MLSTM_CHUNK = 256


def _rms(x, g):
    ms = jnp.mean(x * x, axis=-1, keepdims=True)
    return x * lax.rsqrt(ms + EPS) * g


def _dot(a, b):
    return jnp.dot(a, b, preferred_element_type=jnp.float32)


def _slab_pitch(tm):
    return tm // V7X_SUBLANES + V7X_SUBLANES


def _tile_to_runs(x_ref, slab_ref, dst_ref):
    tm, d = x_ref.shape
    p, pitch = tm // V7X_SUBLANES, _slab_pitch(tm)
    for k in range(d // V7X_LANES):
        lanes = slice(k * V7X_LANES, (k + 1) * V7X_LANES)
        for s in range(V7X_SUBLANES):
            slab_ref[k, s * pitch:s * pitch + p, :] = x_ref[s * p:(s + 1) * p, lanes]
        for j in range(p):
            dst_ref[j * V7X_SUBLANES:(j + 1) * V7X_SUBLANES, lanes] = (
                slab_ref[k, pl.ds(j, V7X_SUBLANES, stride=pitch), :])


def _tile_from_runs(y, slab_ref, o_ref):
    tm, d = y.shape
    p, pitch = tm // V7X_SUBLANES, _slab_pitch(tm)
    for k in range(d // V7X_LANES):
        lanes = slice(k * V7X_LANES, (k + 1) * V7X_LANES)
        for j in range(p):
            slab_ref[k, pl.ds(j, V7X_SUBLANES, stride=pitch), :] = (
                y[j * V7X_SUBLANES:(j + 1) * V7X_SUBLANES, lanes])
        for s in range(V7X_SUBLANES):
            o_ref[s * p:(s + 1) * p, lanes] = slab_ref[k, s * pitch:s * pitch + p, :]


def _causal_conv3(u, prev, cw, cb):
    tm = u.shape[0]
    s8 = V7X_SUBLANES
    first = lax.broadcasted_iota(jnp.int32, (s8, u.shape[1]), 0) == 0

    def run_before(cur_blk, prev_blk):
        return jnp.where(first, pltpu.roll(prev_blk, 1, 0), pltpu.roll(cur_blk, 1, 0))

    h1 = run_before(u[tm - s8:], prev[s8:])
    h2 = run_before(u[tm - 2 * s8:tm - s8], prev[:s8])
    u1 = jnp.concatenate([h1, u[:tm - s8]], axis=0)
    u2 = jnp.concatenate([h2, h1, u[:tm - 2 * s8]], axis=0)
    return cw[0:1] * u2 + cw[1:2] * u1 + cw[2:3] * u + cb


CONV_COLS = V7X_MXU_DIM


def _conv_block_kernel(x_ref, g_ref, wu_ref, cw_ref, cb_ref, wd_ref, gf_ref, o_ref,
                       xn_ref, act_ref, carry_ref, slab_ref, xr_ref, *,
                       kind, tiles_per_seq, final_norm, runs_in, runs_out):
    tm = x_ref.shape[0]
    width = wd_ref.shape[0]
    cc = CONV_COLS

    def conv(u, ccols):
        prev = carry_ref[:, ccols]
        carry_ref[:, ccols] = u[tm - 2 * V7X_SUBLANES:]
        return _causal_conv3(u, prev, cw_ref[:, ccols], cb_ref[:, ccols])

    @pl.when(pl.program_id(0) % tiles_per_seq == 0)
    def _():
        carry_ref[...] = jnp.zeros_like(carry_ref)

    if runs_in:
        x = x_ref[...]
    else:
        _tile_to_runs(x_ref, slab_ref, xr_ref)
        x = xr_ref[...]
    xn_ref[...] = _rms(x, g_ref[...]).astype(jnp.bfloat16)

    for c in range(width // cc):
        xn = xn_ref[...]
        cols = [slice(grp * width + c * cc, grp * width + (c + 1) * cc) for grp in range(3)]
        if kind == "ffn":
            gate, val = [conv(_dot(xn, wu_ref[:, cols[grp]]), cols[grp]) for grp in range(2)]
            act = jax.nn.silu(gate) * val
        else:
            bg, cg, xi = [_dot(xn, wu_ref[:, cols[grp]]) for grp in range(3)]
            act = bg * conv(cg * xi, cols[0])
        act_ref[:, cols[0]] = act.astype(jnp.bfloat16)

    y = x + _dot(act_ref[...], wd_ref[...])
    if final_norm:
        y = _rms(y, gf_ref[...])
    if runs_out:
        o_ref[...] = y
    else:
        _tile_from_runs(y, slab_ref, o_ref)


def _conv_block(x, g, w_up, conv_w, conv_b, w_down, g_final, *, kind, seq_len, final_norm,
                runs_in, runs_out):
    t, d = x.shape
    width = w_down.shape[0]
    n_conv = conv_w.shape[1]
    tm = TOKEN_TILE
    w_up = w_up.astype(jnp.bfloat16)
    const = lambda i: (0, 0)
    once = pl.Buffered(1)
    kern = functools.partial(_conv_block_kernel, kind=kind, tiles_per_seq=seq_len // tm,
                             final_norm=final_norm, runs_in=runs_in, runs_out=runs_out)
    return pl.pallas_call(
        kern,
        grid=(t // tm,),
        in_specs=[
            pl.BlockSpec((tm, d), lambda i: (i, 0)),
            pl.BlockSpec((1, d), const, pipeline_mode=once),
            pl.BlockSpec(w_up.shape, const, pipeline_mode=once),
            pl.BlockSpec(conv_w.shape, const, pipeline_mode=once),
            pl.BlockSpec((1, n_conv), const, pipeline_mode=once),
            pl.BlockSpec(w_down.shape, const, pipeline_mode=once),
            pl.BlockSpec((1, d), const, pipeline_mode=once),
        ],
        out_specs=pl.BlockSpec((tm, d), lambda i: (i, 0)),
        out_shape=jax.ShapeDtypeStruct((t, d), jnp.float32),
        scratch_shapes=[
            pltpu.VMEM((tm, d), jnp.bfloat16),
            pltpu.VMEM((tm, width), jnp.bfloat16),
            pltpu.VMEM((2 * V7X_SUBLANES, n_conv), jnp.float32),
            pltpu.VMEM((d // V7X_LANES, V7X_SUBLANES * _slab_pitch(tm), V7X_LANES),
                       jnp.float32),
            pltpu.VMEM((tm, d), jnp.float32),
        ],
        compiler_params=pltpu.CompilerParams(
            dimension_semantics=("arbitrary",), vmem_limit_bytes=V7X_VMEM_LIMIT_BYTES),
        name=f"{kind}_block",
    )(x, g.reshape(1, d), w_up, conv_w, conv_b.reshape(1, n_conv),
      w_down.astype(jnp.bfloat16), g_final.reshape(1, d))


def _cumsum_rows(x):
    n = x.shape[0]
    rid = lax.broadcasted_iota(jnp.int32, x.shape, 0)
    s = 1
    while s < n:
        x = x + jnp.where(rid >= s, pltpu.roll(x, s, 0), 0.0)
        s *= 2
    return x


def _mlstm_kernel(x_ref, g_ref, win_ref, wg_ref, bg_ref, hn_ref, wo_ref, o_ref,
                  xn_ref, q_ref, k_ref, v_ref, og_ref, gc_ref, gr_ref, hs_ref, hb_ref,
                  c_ref, m_ref, *, heads, dqk, dv, tiles_per_seq):
    tm = x_ref.shape[0]
    L = MLSTM_CHUNK
    qk = heads * dqk
    vw = heads * dv
    dve = dv + V7X_LANES

    @pl.when(pl.program_id(0) % tiles_per_seq == 0)
    def _():
        c_ref[...] = jnp.zeros_like(c_ref)
        m_ref[...] = jnp.zeros_like(m_ref)

    x = x_ref[...]
    xn_ref[...] = _rms(x, g_ref[...]).astype(jnp.bfloat16)
    xn = xn_ref[...]

    gl = _dot(xn, wg_ref[...]) + bg_ref[...]
    lf = jax.nn.log_sigmoid(gl)
    b = jnp.concatenate([_cumsum_rows(lf[c * L:(c + 1) * L]) for c in range(tm // L)], axis=0)
    a = gl - pltpu.roll(b, V7X_LANES - heads, 1)
    lane = lax.broadcasted_iota(jnp.int32, (tm, V7X_LANES), 1)
    gcols = jnp.where(lane < heads, a, b)
    gc_ref[...] = gcols
    gr_ref[...] = gcols.T

    q_ref[...] = (_dot(xn, win_ref[:, 0:qk]) * (dqk ** -0.5)).astype(jnp.bfloat16)
    k_ref[...] = _dot(xn, win_ref[:, qk:2 * qk]).astype(jnp.bfloat16)
    for h in range(heads):
        v_ref[h, :, 0:dv] = _dot(xn, win_ref[:, 2 * qk + h * dv:2 * qk + (h + 1) * dv]).astype(jnp.bfloat16)
        v_ref[h, :, dv:dve] = jnp.ones((tm, V7X_LANES), jnp.bfloat16)
    og_ref[...] = _dot(xn, win_ref[:, 2 * qk + vw:2 * qk + 2 * vw])

    causal = (lax.broadcasted_iota(jnp.int32, (L, L), 0) >= lax.broadcasted_iota(jnp.int32, (L, L), 1))

    rep = lambda t, n: jnp.concatenate([t] * n, axis=1)

    for c in range(tm // L):
        rows = slice(c * L, (c + 1) * L)
        for h in range(heads):
            qh = q_ref[rows, h * dqk:(h + 1) * dqk]
            kh = k_ref[rows, h * dqk:(h + 1) * dqk]
            ve = v_ref[h, rows, :]
            a_rep = jnp.broadcast_to(gc_ref[rows, h:h + 1], (L, V7X_LANES))
            b_rep = jnp.broadcast_to(gc_ref[rows, heads + h:heads + h + 1], (L, V7X_LANES))
            a_row = gr_ref[h:h + 1, rows]
            b_last = gr_ref[heads + h:heads + h + 1, (c + 1) * L - 1:(c + 1) * L]
            m_prev = m_ref[h, 0:1, 0:1]
            cst = c_ref[h]

            log_d = jnp.where(causal, rep(b_rep, L // V7X_LANES) + a_row, -jnp.inf)
            inter = b_rep + m_prev
            m_t = jnp.maximum(inter, jnp.max(log_d, axis=-1, keepdims=True))
            dm = jnp.exp(log_d - rep(m_t, L // V7X_LANES))
            sc = jnp.exp(inter - m_t)
            sk = lax.dot_general(qh, kh, (((1,), (1,)), ((), ())),
                                 preferred_element_type=jnp.float32) * dm
            tot = (_dot(sk.astype(jnp.bfloat16), ve)
                   + rep(sc, dve // V7X_LANES) * _dot(qh, cst.astype(jnp.bfloat16)))
            den = tot[:, dv:]
            inv = 1.0 / jnp.maximum(jnp.abs(den), jnp.exp(-m_t))
            hs_ref[rows, h * dv:(h + 1) * dv] = tot[:, 0:dv] * rep(inv, dv // V7X_LANES)

            m_new = b_last + jnp.maximum(m_prev, jnp.max(a_row, axis=-1, keepdims=True))
            w_rep = jnp.exp(b_last + a_rep - m_new)
            decay = jnp.exp(b_last + m_prev - m_new)
            kw = (kh.astype(jnp.float32) * rep(w_rep, dqk // V7X_LANES)).astype(jnp.bfloat16)
            upd = lax.dot_general(kw, ve, (((0,), (0,)), ((), ())),
                                  preferred_element_type=jnp.float32)
            c_ref[h] = decay * cst + upd
            m_ref[h] = jnp.broadcast_to(m_new, m_ref.shape[1:])

    for h in range(heads):
        hc = slice(h * dv, (h + 1) * dv)
        hs = hs_ref[:, hc]
        hn = _rms(hs, hn_ref[:, hc]) * jax.nn.sigmoid(og_ref[:, hc])
        hb_ref[:, hc] = hn.astype(jnp.bfloat16)
    o_ref[...] = x + _dot(hb_ref[...], wo_ref[...])


def _mlstm_block(x, g, w_in, b_gates, head_norm, w_out, *, seq_len):
    t, d = x.shape
    heads, dv = head_norm.shape
    vw = heads * dv
    qk = (w_in.shape[1] - 2 * vw - 2 * heads) // 2
    dqk = qk // heads
    tm = TOKEN_TILE
    n_main = 2 * qk + 2 * vw
    w_main = w_in[:, :n_main].astype(jnp.bfloat16)
    w_gate = jnp.pad(w_in[:, n_main:], ((0, 0), (0, V7X_LANES - 2 * heads))).astype(jnp.bfloat16)
    b_gate = jnp.pad(b_gates, (0, V7X_LANES - 2 * heads)).reshape(1, V7X_LANES)
    const = lambda i: (0, 0)
    once = pl.Buffered(1)
    kern = functools.partial(_mlstm_kernel, heads=heads, dqk=dqk, dv=dv, tiles_per_seq=seq_len // tm)
    return pl.pallas_call(
        kern,
        grid=(t // tm,),
        in_specs=[
            pl.BlockSpec((tm, d), lambda i: (i, 0)),
            pl.BlockSpec((1, d), const, pipeline_mode=once),
            pl.BlockSpec(w_main.shape, const, pipeline_mode=once),
            pl.BlockSpec(w_gate.shape, const, pipeline_mode=once),
            pl.BlockSpec((1, V7X_LANES), const, pipeline_mode=once),
            pl.BlockSpec((1, vw), const, pipeline_mode=once),
            pl.BlockSpec(w_out.shape, const, pipeline_mode=once),
        ],
        out_specs=pl.BlockSpec((tm, d), lambda i: (i, 0)),
        out_shape=jax.ShapeDtypeStruct((t, d), jnp.float32),
        scratch_shapes=[
            pltpu.VMEM((tm, d), jnp.bfloat16),
            pltpu.VMEM((tm, qk), jnp.bfloat16),
            pltpu.VMEM((tm, qk), jnp.bfloat16),
            pltpu.VMEM((heads, tm, dv + V7X_LANES), jnp.bfloat16),
            pltpu.VMEM((tm, vw), jnp.float32),
            pltpu.VMEM((tm, V7X_LANES), jnp.float32),
            pltpu.VMEM((V7X_LANES, tm), jnp.float32),
            pltpu.VMEM((tm, vw), jnp.float32),
            pltpu.VMEM((tm, vw), jnp.bfloat16),
            pltpu.VMEM((heads, dqk, dv + V7X_LANES), jnp.float32),
            pltpu.VMEM((heads, V7X_SUBLANES, V7X_LANES), jnp.float32),
        ],
        compiler_params=pltpu.CompilerParams(
            dimension_semantics=("arbitrary",), vmem_limit_bytes=V7X_VMEM_LIMIT_BYTES),
        name="mlstm_block",
    )(x, g.reshape(1, d), w_main, w_gate, b_gate, head_norm.reshape(1, vw), w_out.astype(jnp.bfloat16))


def kernel(x, l0_norm_mix, l0_mlstm_w_in, l0_mlstm_b_gates, l0_mlstm_head_norm, l0_mlstm_w_out, l0_norm_ffn, l0_ffn_w_up, l0_ffn_conv_w, l0_ffn_conv_b, l0_ffn_w_down, l1_norm_mix, l1_sconv_w_in, l1_sconv_conv_w, l1_sconv_conv_b, l1_sconv_w_out, l1_norm_ffn, l1_ffn_w_up, l1_ffn_conv_w, l1_ffn_conv_b, l1_ffn_w_down, final_norm):
    bsz, seq_len, d = x.shape
    h = x.reshape(bsz * seq_len, d)
    h = _mlstm_block(h, l0_norm_mix, l0_mlstm_w_in, l0_mlstm_b_gates, l0_mlstm_head_norm,
                     l0_mlstm_w_out, seq_len=seq_len)
    h = _conv_block(h, l0_norm_ffn, l0_ffn_w_up, l0_ffn_conv_w, l0_ffn_conv_b, l0_ffn_w_down,
                    final_norm, kind="ffn", seq_len=seq_len, final_norm=False,
                    runs_in=False, runs_out=True)
    h = _conv_block(h, l1_norm_mix, l1_sconv_w_in, l1_sconv_conv_w, l1_sconv_conv_b, l1_sconv_w_out,
                    final_norm, kind="sconv", seq_len=seq_len, final_norm=False,
                    runs_in=True, runs_out=True)
    h = _conv_block(h, l1_norm_ffn, l1_ffn_w_up, l1_ffn_conv_w, l1_ffn_conv_b, l1_ffn_w_down,
                    final_norm, kind="ffn", seq_len=seq_len, final_norm=True,
                    runs_in=True, runs_out=False)
    return h.reshape(bsz, seq_len, d)
```

```python
import functools

import jax
import jax.numpy as jnp
from jax import lax
from jax.experimental import pallas as pl
from jax.experimental.pallas import tpu as pltpu

EPS = 1e-6
CONV_K = 3

V7X_LANES = 128
V7X_SUBLANES = 8
V7X_MXU_DIM = 256
V7X_VMEM_LIMIT_BYTES = 56 * 1024 * 1024

TOKEN_TILE = 512
MLSTM_CHUNK = 256


def _rms(x, g):
    ms = jnp.mean(x * x, axis=-1, keepdims=True)
    return x * lax.rsqrt(ms + EPS) * g


def _dot(a, b):
    return jnp.dot(a, b, preferred_element_type=jnp.float32)


BF16_SUBLANES = 2 * V7X_SUBLANES


def _cast_specs(ws, n_steps):
    specs, shapes = [], []
    for w in ws:
        rows, cols = w.shape
        blk = min(b for b in range(BF16_SUBLANES, rows + 1, BF16_SUBLANES)
                  if rows % b == 0 and b * n_steps >= rows)
        last = rows // blk - 1
        specs.append(pl.BlockSpec((blk, cols), lambda i, last=last: (jnp.minimum(i, last), 0)))
        shapes.append(jax.ShapeDtypeStruct(w.shape, jnp.bfloat16))
    return specs, shapes


def _with_casts(body, n_in, n_cast):
    def kern(*refs):
        ins, src = refs[:n_in], refs[n_in:n_in + n_cast]
        out, dst = refs[n_in + n_cast], refs[n_in + n_cast + 1:n_in + 2 * n_cast + 1]
        for s, t in zip(src, dst):
            t[...] = s[...].astype(jnp.bfloat16)
        body(*ins, out, *refs[n_in + 2 * n_cast + 1:])
    return kern


def _slab_pitch(tm):
    return tm // V7X_SUBLANES + V7X_SUBLANES


def _tile_to_runs(x_ref, slab_ref, dst_ref):
    tm, d = x_ref.shape
    p, pitch = tm // V7X_SUBLANES, _slab_pitch(tm)
    for k in range(d // V7X_LANES):
        lanes = slice(k * V7X_LANES, (k + 1) * V7X_LANES)
        for s in range(V7X_SUBLANES):
            slab_ref[k, s * pitch:s * pitch + p, :] = x_ref[s * p:(s + 1) * p, lanes]
        for j in range(p):
            dst_ref[j * V7X_SUBLANES:(j + 1) * V7X_SUBLANES, lanes] = (
                slab_ref[k, pl.ds(j, V7X_SUBLANES, stride=pitch), :])


def _tile_from_runs(y, slab_ref, o_ref):
    tm, d = y.shape
    p, pitch = tm // V7X_SUBLANES, _slab_pitch(tm)
    for k in range(d // V7X_LANES):
        lanes = slice(k * V7X_LANES, (k + 1) * V7X_LANES)
        for j in range(p):
            slab_ref[k, pl.ds(j, V7X_SUBLANES, stride=pitch), :] = (
                y[j * V7X_SUBLANES:(j + 1) * V7X_SUBLANES, lanes])
        for s in range(V7X_SUBLANES):
            o_ref[s * p:(s + 1) * p, lanes] = slab_ref[k, s * pitch:s * pitch + p, :]


def _causal_conv3(u, prev, cw, cb):
    tm = u.shape[0]
    s8 = V7X_SUBLANES
    first = lax.broadcasted_iota(jnp.int32, (s8, u.shape[1]), 0) == 0

    def run_before(cur_blk, prev_blk):
        return jnp.where(first, pltpu.roll(prev_blk, 1, 0), pltpu.roll(cur_blk, 1, 0))

    h1 = run_before(u[tm - s8:], prev[s8:])
    h2 = run_before(u[tm - 2 * s8:tm - s8], prev[:s8])
    u1 = jnp.concatenate([h1, u[:tm - s8]], axis=0)
    u2 = jnp.concatenate([h2, h1, u[:tm - 2 * s8]], axis=0)
    return cw[0:1] * u2 + cw[1:2] * u1 + cw[2:3] * u + cb


CONV_COLS = V7X_MXU_DIM


def _conv_block_kernel(x_ref, g_ref, wu_ref, cw_ref, cb_ref, wd_ref, gf_ref, o_ref,
                       xn_ref, act_ref, carry_ref, slab_ref, xr_ref, *,
                       kind, tiles_per_seq, final_norm, runs_in, runs_out):
    tm = x_ref.shape[0]
    width = wd_ref.shape[0]
    cc = CONV_COLS

    def conv(u, ccols):
        prev = carry_ref[:, ccols]
        carry_ref[:, ccols] = u[tm - 2 * V7X_SUBLANES:]
        return _causal_conv3(u, prev, cw_ref[:, ccols], cb_ref[:, ccols])

    @pl.when(pl.program_id(0) % tiles_per_seq == 0)
    def _():
        carry_ref[...] = jnp.zeros_like(carry_ref)

    if runs_in:
        x = x_ref[...]
    else:
        _tile_to_runs(x_ref, slab_ref, xr_ref)
        x = xr_ref[...]
    xn_ref[...] = _rms(x, g_ref[...]).astype(jnp.bfloat16)

    for c in range(width // cc):
        xn = xn_ref[...]
        cols = [slice(grp * width + c * cc, grp * width + (c + 1) * cc) for grp in range(3)]
        if kind == "ffn":
            gate, val = [conv(_dot(xn, wu_ref[:, cols[grp]]), cols[grp]) for grp in range(2)]
            act = jax.nn.silu(gate) * val
        else:
            bg, cg, xi = [_dot(xn, wu_ref[:, cols[grp]]) for grp in range(3)]
            act = bg * conv(cg * xi, cols[0])
        act_ref[:, cols[0]] = act.astype(jnp.bfloat16)

    y = x + _dot(act_ref[...], wd_ref[...])
    if final_norm:
        y = _rms(y, gf_ref[...])
    if runs_out:
        o_ref[...] = y
    else:
        _tile_from_runs(y, slab_ref, o_ref)


def _conv_block(x, g, w_up, conv_w, conv_b, w_down, g_final, *, kind, seq_len, final_norm,
                runs_in, runs_out, cast_next=()):
    t, d = x.shape
    width = w_down.shape[0]
    n_conv = conv_w.shape[1]
    tm = TOKEN_TILE
    w_up = w_up.astype(jnp.bfloat16)
    const = lambda i: (0, 0)
    once = pl.Buffered(1)
    body = functools.partial(_conv_block_kernel, kind=kind, tiles_per_seq=seq_len // tm,
                             final_norm=final_norm, runs_in=runs_in, runs_out=runs_out)
    cast_specs, cast_shapes = _cast_specs(cast_next, t // tm)
    in_specs = [
        pl.BlockSpec((tm, d), lambda i: (i, 0)),
        pl.BlockSpec((1, d), const, pipeline_mode=once),
        pl.BlockSpec(w_up.shape, const, pipeline_mode=once),
        pl.BlockSpec(conv_w.shape, const, pipeline_mode=once),
        pl.BlockSpec((1, n_conv), const, pipeline_mode=once),
        pl.BlockSpec(w_down.shape, const, pipeline_mode=once),
        pl.BlockSpec((1, d), const, pipeline_mode=once),
    ]
    return pl.pallas_call(
        _with_casts(body, len(in_specs), len(cast_next)),
        grid=(t // tm,),
        in_specs=in_specs + cast_specs,
        out_specs=[pl.BlockSpec((tm, d), lambda i: (i, 0))] + cast_specs,
        out_shape=[jax.ShapeDtypeStruct((t, d), jnp.float32)] + cast_shapes,
        scratch_shapes=[
            pltpu.VMEM((tm, d), jnp.bfloat16),
            pltpu.VMEM((tm, width), jnp.bfloat16),
            pltpu.VMEM((2 * V7X_SUBLANES, n_conv), jnp.float32),
            pltpu.VMEM((d // V7X_LANES, V7X_SUBLANES * _slab_pitch(tm), V7X_LANES),
                       jnp.float32),
            pltpu.VMEM((tm, d), jnp.float32),
        ],
        compiler_params=pltpu.CompilerParams(
            dimension_semantics=("arbitrary",), vmem_limit_bytes=V7X_VMEM_LIMIT_BYTES),
        name=f"{kind}_block",
    )(x, g.reshape(1, d), w_up, conv_w, conv_b.reshape(1, n_conv),
      w_down.astype(jnp.bfloat16), g_final.reshape(1, d), *cast_next)


def _cumsum_rows(x):
    n = x.shape[0]
    rid = lax.broadcasted_iota(jnp.int32, x.shape, 0)
    s = 1
    while s < n:
        x = x + jnp.where(rid >= s, pltpu.roll(x, s, 0), 0.0)
        s *= 2
    return x


def _mlstm_kernel(x_ref, g_ref, win_ref, wg_ref, bg_ref, hn_ref, wo_ref, o_ref,
                  xn_ref, q_ref, k_ref, v_ref, og_ref, gc_ref, gr_ref, hs_ref, hb_ref,
                  c_ref, m_ref, *, heads, dqk, dv, tiles_per_seq):
    tm = x_ref.shape[0]
    L = MLSTM_CHUNK
    qk = heads * dqk
    vw = heads * dv
    dve = dv + V7X_LANES

    @pl.when(pl.program_id(0) % tiles_per_seq == 0)
    def _():
        c_ref[...] = jnp.zeros_like(c_ref)
        m_ref[...] = jnp.zeros_like(m_ref)

    x = x_ref[...]
    xn_ref[...] = _rms(x, g_ref[...]).astype(jnp.bfloat16)
    xn = xn_ref[...]

    gl = _dot(xn, wg_ref[...]) + bg_ref[...]
    lf = jax.nn.log_sigmoid(gl)
    b = jnp.concatenate([_cumsum_rows(lf[c * L:(c + 1) * L]) for c in range(tm // L)], axis=0)
    a = gl - pltpu.roll(b, V7X_LANES - heads, 1)
    lane = lax.broadcasted_iota(jnp.int32, (tm, V7X_LANES), 1)
    gcols = jnp.where(lane < heads, a, b)
    gc_ref[...] = gcols
    gr_ref[...] = gcols.T

    q_ref[...] = (_dot(xn, win_ref[:, 0:qk]) * (dqk ** -0.5)).astype(jnp.bfloat16)
    k_ref[...] = _dot(xn, win_ref[:, qk:2 * qk]).astype(jnp.bfloat16)
    for h in range(heads):
        v_ref[h, :, 0:dv] = _dot(xn, win_ref[:, 2 * qk + h * dv:2 * qk + (h + 1) * dv]).astype(jnp.bfloat16)
        v_ref[h, :, dv:dve] = jnp.ones((tm, V7X_LANES), jnp.bfloat16)
    og_ref[...] = _dot(xn, win_ref[:, 2 * qk + vw:2 * qk + 2 * vw])

    causal = (lax.broadcasted_iota(jnp.int32, (L, L), 0) >= lax.broadcasted_iota(jnp.int32, (L, L), 1))

    rep = lambda t, n: jnp.concatenate([t] * n, axis=1)

    for c in range(tm // L):
        rows = slice(c * L, (c + 1) * L)
        for h in range(heads):
            qh = q_ref[rows, h * dqk:(h + 1) * dqk]
            kh = k_ref[rows, h * dqk:(h + 1) * dqk]
            ve = v_ref[h, rows, :]
            b_rep = jnp.broadcast_to(gc_ref[rows, heads + h:heads + h + 1], (L, V7X_LANES))
            a_row = gr_ref[h:h + 1, rows]
            b_last = gr_ref[heads + h:heads + h + 1, (c + 1) * L - 1:(c + 1) * L]
            m_prev = m_ref[h, 0:1, 0:1]
            cst = c_ref[h]

            log_d = jnp.where(causal, rep(b_rep, L // V7X_LANES) + a_row, -jnp.inf)
            inter = b_rep + m_prev
            m_t = jnp.maximum(inter, jnp.max(log_d, axis=-1, keepdims=True))
            dm = jnp.exp(log_d - rep(m_t, L // V7X_LANES))
            sc = jnp.exp(inter - m_t)
            sk = lax.dot_general(qh, kh, (((1,), (1,)), ((), ())),
                                 preferred_element_type=jnp.float32) * dm
            tot = (_dot(sk.astype(jnp.bfloat16), ve)
                   + rep(sc, dve // V7X_LANES) * _dot(qh, cst.astype(jnp.bfloat16)))
            den = tot[:, dv:]
            inv = 1.0 / jnp.maximum(jnp.abs(den), jnp.exp(-m_t))
            hs_ref[rows, h * dv:(h + 1) * dv] = tot[:, 0:dv] * rep(inv, dv // V7X_LANES)

            m_new = b_last + jnp.maximum(m_prev, jnp.max(a_row, axis=-1, keepdims=True))
            w_row = jnp.exp(b_last + a_row - m_new)
            decay = jnp.exp(b_last + m_prev - m_new)
            kw_t = (kh.T.astype(jnp.float32) * w_row).astype(jnp.bfloat16)
            upd = _dot(kw_t, ve)
            c_ref[h] = decay * cst + upd
            m_ref[h] = jnp.broadcast_to(m_new, m_ref.shape[1:])

    for h in range(heads):
        hc = slice(h * dv, (h + 1) * dv)
        hs = hs_ref[:, hc]
        hn = _rms(hs, hn_ref[:, hc]) * jax.nn.sigmoid(og_ref[:, hc])
        hb_ref[:, hc] = hn.astype(jnp.bfloat16)
    o_ref[...] = x + _dot(hb_ref[...], wo_ref[...])


def _mlstm_block(x, g, w_in, b_gates, head_norm, w_out, *, seq_len, cast_next=()):
    t, d = x.shape
    heads, dv = head_norm.shape
    vw = heads * dv
    qk = (w_in.shape[1] - 2 * vw - 2 * heads) // 2
    dqk = qk // heads
    tm = TOKEN_TILE
    n_main = 2 * qk + 2 * vw
    w_bf16 = w_in.astype(jnp.bfloat16)
    w_gate = jnp.pad(w_bf16[:, n_main:], ((0, 0), (0, V7X_LANES - 2 * heads)))
    b_gate = jnp.pad(b_gates, (0, V7X_LANES - 2 * heads)).reshape(1, V7X_LANES)
    const = lambda i: (0, 0)
    once = pl.Buffered(1)
    body = functools.partial(_mlstm_kernel, heads=heads, dqk=dqk, dv=dv, tiles_per_seq=seq_len // tm)
    cast_specs, cast_shapes = _cast_specs(cast_next, t // tm)
    in_specs = [
        pl.BlockSpec((tm, d), lambda i: (i, 0)),
        pl.BlockSpec((1, d), const, pipeline_mode=once),
        pl.BlockSpec((d, n_main), const, pipeline_mode=once),
        pl.BlockSpec(w_gate.shape, const, pipeline_mode=once),
        pl.BlockSpec((1, V7X_LANES), const, pipeline_mode=once),
        pl.BlockSpec((1, vw), const, pipeline_mode=once),
        pl.BlockSpec(w_out.shape, const, pipeline_mode=once),
    ]
    return pl.pallas_call(
        _with_casts(body, len(in_specs), len(cast_next)),
        grid=(t // tm,),
        in_specs=in_specs + cast_specs,
        out_specs=[pl.BlockSpec((tm, d), lambda i: (i, 0))] + cast_specs,
        out_shape=[jax.ShapeDtypeStruct((t, d), jnp.float32)] + cast_shapes,
        scratch_shapes=[
            pltpu.VMEM((tm, d), jnp.bfloat16),
            pltpu.VMEM((tm, qk), jnp.bfloat16),
            pltpu.VMEM((tm, qk), jnp.bfloat16),
            pltpu.VMEM((heads, tm, dv + V7X_LANES), jnp.bfloat16),
            pltpu.VMEM((tm, vw), jnp.float32),
            pltpu.VMEM((tm, V7X_LANES), jnp.float32),
            pltpu.VMEM((V7X_LANES, tm), jnp.float32),
            pltpu.VMEM((tm, vw), jnp.float32),
            pltpu.VMEM((tm, vw), jnp.bfloat16),
            pltpu.VMEM((heads, dqk, dv + V7X_LANES), jnp.float32),
            pltpu.VMEM((heads, V7X_SUBLANES, V7X_LANES), jnp.float32),
        ],
        compiler_params=pltpu.CompilerParams(
            dimension_semantics=("arbitrary",), vmem_limit_bytes=V7X_VMEM_LIMIT_BYTES),
        name="mlstm_block",
    )(x, g.reshape(1, d), w_bf16, w_gate, b_gate, head_norm.reshape(1, vw), w_out.astype(jnp.bfloat16),
      *cast_next)


def kernel(x, l0_norm_mix, l0_mlstm_w_in, l0_mlstm_b_gates, l0_mlstm_head_norm, l0_mlstm_w_out, l0_norm_ffn, l0_ffn_w_up, l0_ffn_conv_w, l0_ffn_conv_b, l0_ffn_w_down, l1_norm_mix, l1_sconv_w_in, l1_sconv_conv_w, l1_sconv_conv_b, l1_sconv_w_out, l1_norm_ffn, l1_ffn_w_up, l1_ffn_conv_w, l1_ffn_conv_b, l1_ffn_w_down, final_norm):
    bsz, seq_len, d = x.shape
    h = x.reshape(bsz * seq_len, d)
    h, w_up, w_down = _mlstm_block(h, l0_norm_mix, l0_mlstm_w_in, l0_mlstm_b_gates,
                                   l0_mlstm_head_norm, l0_mlstm_w_out, seq_len=seq_len,
                                   cast_next=(l0_ffn_w_up, l0_ffn_w_down))
    h, w_up, w_down = _conv_block(h, l0_norm_ffn, w_up, l0_ffn_conv_w, l0_ffn_conv_b, w_down,
                                  final_norm, kind="ffn", seq_len=seq_len, final_norm=False,
                                  runs_in=False, runs_out=True,
                                  cast_next=(l1_sconv_w_in, l1_sconv_w_out))
    h, w_up, w_down = _conv_block(h, l1_norm_mix, w_up, l1_sconv_conv_w, l1_sconv_conv_b, w_down,
                                  final_norm, kind="sconv", seq_len=seq_len, final_norm=False,
                                  runs_in=True, runs_out=True,
                                  cast_next=(l1_ffn_w_up, l1_ffn_w_down))
    (h,) = _conv_block(h, l1_norm_ffn, w_up, l1_ffn_conv_w, l1_ffn_conv_b, w_down,
                       final_norm, kind="ffn", seq_len=seq_len, final_norm=True,
                       runs_in=True, runs_out=False)
    return h.reshape(bsz, seq_len, d)
```

```python
import functools

import jax
import jax.numpy as jnp
from jax import lax
from jax.experimental import pallas as pl
from jax.experimental.pallas import tpu as pltpu

EPS = 1e-6
CONV_K = 3

V7X_LANES = 128
V7X_SUBLANES = 8
V7X_MXU_DIM = 256
V7X_VMEM_LIMIT_BYTES = 56 * 1024 * 1024

CONV_TILE = {"ffn": 1024, "sconv": 1024}
MLSTM_TILE = 1024
MLSTM_CHUNK = 256


def _rms(x, g):
    ms = jnp.mean(x * x, axis=-1, keepdims=True)
    return x * lax.rsqrt(ms + EPS) * g


def _dot(a, b):
    return jnp.dot(a, b, preferred_element_type=jnp.float32)


BF16_SUBLANES = 2 * V7X_SUBLANES


def _cast_specs(ws, n_steps):
    specs, shapes = [], []
    for w in ws:
        rows, cols = w.shape
        blk = min(b for b in range(BF16_SUBLANES, rows + 1, BF16_SUBLANES)
                  if rows % b == 0 and b * n_steps >= rows)
        last = rows // blk - 1
        specs.append(pl.BlockSpec((blk, cols), lambda i, last=last: (jnp.minimum(i, last), 0)))
        shapes.append(jax.ShapeDtypeStruct(w.shape, jnp.bfloat16))
    return specs, shapes


def _with_casts(body, n_in, n_cast):
    def kern(*refs):
        ins, src = refs[:n_in], refs[n_in:n_in + n_cast]
        out, dst = refs[n_in + n_cast], refs[n_in + n_cast + 1:n_in + 2 * n_cast + 1]
        for s, t in zip(src, dst):
            t[...] = s[...].astype(jnp.bfloat16)
        body(*ins, out, *refs[n_in + 2 * n_cast + 1:])
    return kern


def _slab_pitch(tm):
    return tm // V7X_SUBLANES + V7X_SUBLANES


def _tile_to_runs(x_ref, slab_ref, dst_ref):
    tm, d = x_ref.shape
    p, pitch = tm // V7X_SUBLANES, _slab_pitch(tm)
    for k in range(d // V7X_LANES):
        lanes = slice(k * V7X_LANES, (k + 1) * V7X_LANES)
        for s in range(V7X_SUBLANES):
            slab_ref[k, s * pitch:s * pitch + p, :] = x_ref[s * p:(s + 1) * p, lanes]
        for j in range(p):
            dst_ref[j * V7X_SUBLANES:(j + 1) * V7X_SUBLANES, lanes] = (
                slab_ref[k, pl.ds(j, V7X_SUBLANES, stride=pitch), :])


def _tile_from_runs(y, slab_ref, o_ref):
    tm, d = y.shape
    p, pitch = tm // V7X_SUBLANES, _slab_pitch(tm)
    for k in range(d // V7X_LANES):
        lanes = slice(k * V7X_LANES, (k + 1) * V7X_LANES)
        for j in range(p):
            slab_ref[k, pl.ds(j, V7X_SUBLANES, stride=pitch), :] = (
                y[j * V7X_SUBLANES:(j + 1) * V7X_SUBLANES, lanes])
        for s in range(V7X_SUBLANES):
            o_ref[s * p:(s + 1) * p, lanes] = slab_ref[k, s * pitch:s * pitch + p, :]


def _causal_conv3(u, prev, cw, cb):
    tm = u.shape[0]
    s8 = V7X_SUBLANES
    first = lax.broadcasted_iota(jnp.int32, (s8, u.shape[1]), 0) == 0

    def run_before(cur_blk, prev_blk):
        return jnp.where(first, pltpu.roll(prev_blk, 1, 0), pltpu.roll(cur_blk, 1, 0))

    h1 = run_before(u[tm - s8:], prev[s8:])
    h2 = run_before(u[tm - 2 * s8:tm - s8], prev[:s8])
    u1 = jnp.concatenate([h1, u[:tm - s8]], axis=0)
    u2 = jnp.concatenate([h2, h1, u[:tm - 2 * s8]], axis=0)
    return cw[0:1] * u2 + cw[1:2] * u1 + cw[2:3] * u + cb


CONV_COLS = V7X_MXU_DIM


def _conv_block_kernel(x_ref, g_ref, wu_ref, cw_ref, cb_ref, wd_ref, gf_ref, o_ref,
                       xn_ref, act_ref, carry_ref, slab_ref, xr_ref, *,
                       kind, tiles_per_seq, final_norm, runs_in, runs_out):
    tm = x_ref.shape[0]
    width = wd_ref.shape[0]
    cc = CONV_COLS

    def conv(u, ccols):
        prev = carry_ref[:, ccols]
        carry_ref[:, ccols] = u[tm - 2 * V7X_SUBLANES:]
        return _causal_conv3(u, prev, cw_ref[:, ccols], cb_ref[:, ccols])

    @pl.when(pl.program_id(0) % tiles_per_seq == 0)
    def _():
        carry_ref[...] = jnp.zeros_like(carry_ref)

    if runs_in:
        x = x_ref[...]
    else:
        _tile_to_runs(x_ref, slab_ref, xr_ref)
        x = xr_ref[...]
    xn_ref[...] = _rms(x, g_ref[...]).astype(jnp.bfloat16)

    for c in range(width // cc):
        xn = xn_ref[...]
        cols = [slice(grp * width + c * cc, grp * width + (c + 1) * cc) for grp in range(3)]
        if kind == "ffn":
            gate, val = [conv(_dot(xn, wu_ref[:, cols[grp]]), cols[grp]) for grp in range(2)]
            act = jax.nn.silu(gate) * val
        else:
            bg, cg, xi = [_dot(xn, wu_ref[:, cols[grp]]) for grp in range(3)]
            act = bg * conv(cg * xi, cols[0])
        act_ref[:, cols[0]] = act.astype(jnp.bfloat16)

    y = x + _dot(act_ref[...], wd_ref[...])
    if final_norm:
        y = _rms(y, gf_ref[...])
    if runs_out:
        o_ref[...] = y
    else:
        _tile_from_runs(y, slab_ref, o_ref)


def _conv_block(x, g, w_up, conv_w, conv_b, w_down, g_final, *, kind, seq_len, final_norm,
                runs_in, runs_out, cast_next=()):
    t, d = x.shape
    width = w_down.shape[0]
    n_conv = conv_w.shape[1]
    tm = CONV_TILE[kind]
    w_up = w_up.astype(jnp.bfloat16)
    const = lambda i: (0, 0)
    once = pl.Buffered(1)
    body = functools.partial(_conv_block_kernel, kind=kind, tiles_per_seq=seq_len // tm,
                             final_norm=final_norm, runs_in=runs_in, runs_out=runs_out)
    cast_specs, cast_shapes = _cast_specs(cast_next, t // tm)
    in_specs = [
        pl.BlockSpec((tm, d), lambda i: (i, 0)),
        pl.BlockSpec((1, d), const, pipeline_mode=once),
        pl.BlockSpec(w_up.shape, const, pipeline_mode=once),
        pl.BlockSpec(conv_w.shape, const, pipeline_mode=once),
        pl.BlockSpec((1, n_conv), const, pipeline_mode=once),
        pl.BlockSpec(w_down.shape, const, pipeline_mode=once),
        pl.BlockSpec((1, d), const, pipeline_mode=once),
    ]
    return pl.pallas_call(
        _with_casts(body, len(in_specs), len(cast_next)),
        grid=(t // tm,),
        in_specs=in_specs + cast_specs,
        out_specs=[pl.BlockSpec((tm, d), lambda i: (i, 0))] + cast_specs,
        out_shape=[jax.ShapeDtypeStruct((t, d), jnp.float32)] + cast_shapes,
        scratch_shapes=[
            pltpu.VMEM((tm, d), jnp.bfloat16),
            pltpu.VMEM((tm, width), jnp.bfloat16),
            pltpu.VMEM((2 * V7X_SUBLANES, n_conv), jnp.float32),
            pltpu.VMEM((d // V7X_LANES, V7X_SUBLANES * _slab_pitch(tm), V7X_LANES),
                       jnp.float32),
            pltpu.VMEM((tm, d), jnp.float32),
        ],
        compiler_params=pltpu.CompilerParams(
            dimension_semantics=("arbitrary",), vmem_limit_bytes=V7X_VMEM_LIMIT_BYTES),
        name=f"{kind}_block",
    )(x, g.reshape(1, d), w_up, conv_w, conv_b.reshape(1, n_conv),
      w_down.astype(jnp.bfloat16), g_final.reshape(1, d), *cast_next)


def _cumsum_rows(x):
    n = x.shape[0]
    rid = lax.broadcasted_iota(jnp.int32, x.shape, 0)
    s = 1
    while s < n:
        x = x + jnp.where(rid >= s, pltpu.roll(x, s, 0), 0.0)
        s *= 2
    return x


def _mlstm_kernel(x_ref, g_ref, win_ref, wg_ref, bg_ref, hn_ref, wo_ref, o_ref,
                  xn_ref, q_ref, k_ref, v_ref, og_ref, gc_ref, gr_ref, hs_ref, hb_ref,
                  c_ref, m_ref, *, heads, dqk, dv, tiles_per_seq):
    tm = x_ref.shape[0]
    L = MLSTM_CHUNK
    qk = heads * dqk
    vw = heads * dv
    dve = dv + V7X_LANES

    @pl.when(pl.program_id(0) % tiles_per_seq == 0)
    def _():
        c_ref[...] = jnp.zeros_like(c_ref)
        m_ref[...] = jnp.zeros_like(m_ref)

    x = x_ref[...]
    xn_ref[...] = _rms(x, g_ref[...]).astype(jnp.bfloat16)
    xn = xn_ref[...]

    gl = _dot(xn, wg_ref[...]) + bg_ref[...]
    lf = jax.nn.log_sigmoid(gl)
    b = jnp.concatenate([_cumsum_rows(lf[c * L:(c + 1) * L]) for c in range(tm // L)], axis=0)
    a = gl - pltpu.roll(b, V7X_LANES - heads, 1)
    lane = lax.broadcasted_iota(jnp.int32, (tm, V7X_LANES), 1)
    gcols = jnp.where(lane < heads, a, b)
    gc_ref[...] = gcols
    gr_ref[...] = gcols.T

    q_ref[...] = (_dot(xn, win_ref[:, 0:qk]) * (dqk ** -0.5)).astype(jnp.bfloat16)
    k_ref[...] = _dot(xn, win_ref[:, qk:2 * qk]).astype(jnp.bfloat16)
    for h in range(heads):
        v_ref[h, :, 0:dv] = _dot(xn, win_ref[:, 2 * qk + h * dv:2 * qk + (h + 1) * dv]).astype(jnp.bfloat16)
        v_ref[h, :, dv:dve] = jnp.ones((tm, V7X_LANES), jnp.bfloat16)
    og_ref[...] = _dot(xn, win_ref[:, 2 * qk + vw:2 * qk + 2 * vw])

    causal = (lax.broadcasted_iota(jnp.int32, (L, L), 0) >= lax.broadcasted_iota(jnp.int32, (L, L), 1))

    rep = lambda t, n: jnp.concatenate([t] * n, axis=1)

    for c in range(tm // L):
        rows = slice(c * L, (c + 1) * L)
        for h in range(heads):
            qh = q_ref[rows, h * dqk:(h + 1) * dqk]
            kh = k_ref[rows, h * dqk:(h + 1) * dqk]
            ve = v_ref[h, rows, :]
            b_rep = jnp.broadcast_to(gc_ref[rows, heads + h:heads + h + 1], (L, V7X_LANES))
            a_row = gr_ref[h:h + 1, rows]
            b_last = gr_ref[heads + h:heads + h + 1, (c + 1) * L - 1:(c + 1) * L]
            m_prev = m_ref[h, 0:1, 0:1]
            cst = c_ref[h]

            log_d = jnp.where(causal, rep(b_rep, L // V7X_LANES) + a_row, -jnp.inf)
            inter = b_rep + m_prev
            m_t = jnp.maximum(inter, jnp.max(log_d, axis=-1, keepdims=True))
            dm = jnp.exp(log_d - rep(m_t, L // V7X_LANES))
            sc = jnp.exp(inter - m_t)
            sk = lax.dot_general(qh, kh, (((1,), (1,)), ((), ())),
                                 preferred_element_type=jnp.float32) * dm
            tot = (_dot(sk.astype(jnp.bfloat16), ve)
                   + rep(sc, dve // V7X_LANES) * _dot(qh, cst.astype(jnp.bfloat16)))
            den = tot[:, dv:]
            inv = 1.0 / jnp.maximum(jnp.abs(den), jnp.exp(-m_t))
            hs_ref[rows, h * dv:(h + 1) * dv] = tot[:, 0:dv] * rep(inv, dv // V7X_LANES)

            m_new = b_last + jnp.maximum(m_prev, jnp.max(a_row, axis=-1, keepdims=True))
            w_row = jnp.exp(b_last + a_row - m_new)
            decay = jnp.exp(b_last + m_prev - m_new)
            kw_t = (kh.T.astype(jnp.float32) * w_row).astype(jnp.bfloat16)
            upd = _dot(kw_t, ve)
            c_ref[h] = decay * cst + upd
            m_ref[h] = jnp.broadcast_to(m_new, m_ref.shape[1:])

    for h in range(heads):
        hc = slice(h * dv, (h + 1) * dv)
        hs = hs_ref[:, hc]
        hn = _rms(hs, hn_ref[:, hc]) * jax.nn.sigmoid(og_ref[:, hc])
        hb_ref[:, hc] = hn.astype(jnp.bfloat16)
    o_ref[...] = x + _dot(hb_ref[...], wo_ref[...])


def _mlstm_block(x, g, w_in, b_gates, head_norm, w_out, *, seq_len, cast_next=()):
    t, d = x.shape
    heads, dv = head_norm.shape
    vw = heads * dv
    qk = (w_in.shape[1] - 2 * vw - 2 * heads) // 2
    dqk = qk // heads
    tm = MLSTM_TILE
    n_main = 2 * qk + 2 * vw
    w_bf16 = w_in.astype(jnp.bfloat16)
    w_gate = jnp.pad(w_bf16[:, n_main:], ((0, 0), (0, V7X_LANES - 2 * heads)))
    b_gate = jnp.pad(b_gates, (0, V7X_LANES - 2 * heads)).reshape(1, V7X_LANES)
    const = lambda i: (0, 0)
    once = pl.Buffered(1)
    body = functools.partial(_mlstm_kernel, heads=heads, dqk=dqk, dv=dv, tiles_per_seq=seq_len // tm)
    cast_specs, cast_shapes = _cast_specs(cast_next, t // tm)
    in_specs = [
        pl.BlockSpec((tm, d), lambda i: (i, 0)),
        pl.BlockSpec((1, d), const, pipeline_mode=once),
        pl.BlockSpec((d, n_main), const, pipeline_mode=once),
        pl.BlockSpec(w_gate.shape, const, pipeline_mode=once),
        pl.BlockSpec((1, V7X_LANES), const, pipeline_mode=once),
        pl.BlockSpec((1, vw), const, pipeline_mode=once),
        pl.BlockSpec(w_out.shape, const, pipeline_mode=once),
    ]
    return pl.pallas_call(
        _with_casts(body, len(in_specs), len(cast_next)),
        grid=(t // tm,),
        in_specs=in_specs + cast_specs,
        out_specs=[pl.BlockSpec((tm, d), lambda i: (i, 0))] + cast_specs,
        out_shape=[jax.ShapeDtypeStruct((t, d), jnp.float32)] + cast_shapes,
        scratch_shapes=[
            pltpu.VMEM((tm, d), jnp.bfloat16),
            pltpu.VMEM((tm, qk), jnp.bfloat16),
            pltpu.VMEM((tm, qk), jnp.bfloat16),
            pltpu.VMEM((heads, tm, dv + V7X_LANES), jnp.bfloat16),
            pltpu.VMEM((tm, vw), jnp.float32),
            pltpu.VMEM((tm, V7X_LANES), jnp.float32),
            pltpu.VMEM((V7X_LANES, tm), jnp.float32),
            pltpu.VMEM((tm, vw), jnp.float32),
            pltpu.VMEM((tm, vw), jnp.bfloat16),
            pltpu.VMEM((heads, dqk, dv + V7X_LANES), jnp.float32),
            pltpu.VMEM((heads, V7X_SUBLANES, V7X_LANES), jnp.float32),
        ],
        compiler_params=pltpu.CompilerParams(
            dimension_semantics=("arbitrary",), vmem_limit_bytes=V7X_VMEM_LIMIT_BYTES),
        name="mlstm_block",
    )(x, g.reshape(1, d), w_bf16, w_gate, b_gate, head_norm.reshape(1, vw), w_out.astype(jnp.bfloat16),
      *cast_next)


def kernel(x, l0_norm_mix, l0_mlstm_w_in, l0_mlstm_b_gates, l0_mlstm_head_norm, l0_mlstm_w_out, l0_norm_ffn, l0_ffn_w_up, l0_ffn_conv_w, l0_ffn_conv_b, l0_ffn_w_down, l1_norm_mix, l1_sconv_w_in, l1_sconv_conv_w, l1_sconv_conv_b, l1_sconv_w_out, l1_norm_ffn, l1_ffn_w_up, l1_ffn_conv_w, l1_ffn_conv_b, l1_ffn_w_down, final_norm):
    bsz, seq_len, d = x.shape
    h = x.reshape(bsz * seq_len, d)
    h, w_up, w_down = _mlstm_block(h, l0_norm_mix, l0_mlstm_w_in, l0_mlstm_b_gates,
                                   l0_mlstm_head_norm, l0_mlstm_w_out, seq_len=seq_len,
                                   cast_next=(l0_ffn_w_up, l0_ffn_w_down))
    h, w_up, w_down = _conv_block(h, l0_norm_ffn, w_up, l0_ffn_conv_w, l0_ffn_conv_b, w_down,
                                  final_norm, kind="ffn", seq_len=seq_len, final_norm=False,
                                  runs_in=False, runs_out=True,
                                  cast_next=(l1_sconv_w_in, l1_sconv_w_out))
    h, w_up, w_down = _conv_block(h, l1_norm_mix, w_up, l1_sconv_conv_w, l1_sconv_conv_b, w_down,
                                  final_norm, kind="sconv", seq_len=seq_len, final_norm=False,
                                  runs_in=True, runs_out=True,
                                  cast_next=(l1_ffn_w_up, l1_ffn_w_down))
    (h,) = _conv_block(h, l1_norm_ffn, w_up, l1_ffn_conv_w, l1_ffn_conv_b, w_down,
                       final_norm, kind="ffn", seq_len=seq_len, final_norm=True,
                       runs_in=True, runs_out=False)
    return h.reshape(bsz, seq_len, d)
```
